```python
import math
import jax, jax.numpy as jnp
from jax import lax
import numpy as np

D_MODEL = 4096
BATCH = 2
SEQ = 8192
DEPTH = 1

RET_HEADS = 16
RET_DK = 128
RET_DV = 256
RET_CHUNK = 128
RET_QK = RET_HEADS * RET_DK
RET_V = RET_HEADS * RET_DV
SWA_Q_HEADS = 64
SWA_KV_HEADS = 8
SWA_HEAD_DIM = 64
SWA_WINDOW = 128
SWA_BLOCK = 128
SWA_Q = SWA_Q_HEADS * SWA_HEAD_DIM
SWA_KV = SWA_KV_HEADS * SWA_HEAD_DIM
MEM_LEN = 256
XATTN_HEADS = 4
XATTN_HEAD_DIM = 256
XATTN_W = XATTN_HEADS * XATTN_HEAD_DIM
N_GROUPS = 8
EXPERTS_PER_GROUP = 8
N_EXPERTS = N_GROUPS * EXPERTS_PER_GROUP
TOP_K = 2
D_EXPERT = 512
MOE_BLOCK = 128
RMS_EPS = 1e-6
ROPE_BASE = 10000.0
NEG_INF = -1e30

IN_WIDTHS = [RET_QK, RET_QK, RET_V, RET_V, SWA_Q, SWA_KV, SWA_KV, D_MODEL, D_MODEL]
IN_WIDTH = sum(IN_WIDTHS)
IN_SPLITS = [int(v) for v in np.cumsum(IN_WIDTHS)[:-1]]

kernel_name = "hybrid_retention_swa_sink_hmoe_block"


def rms_norm(x, g):
    xf = x.astype(jnp.float32)
    y = xf * lax.rsqrt(jnp.mean(xf * xf, axis=-1, keepdims=True) + RMS_EPS)
    return (y * g.astype(jnp.float32)).astype(x.dtype)


def rotary(x, pos):
    half = x.shape[-1] // 2
    inv = ROPE_BASE ** (-jnp.arange(half, dtype=jnp.float32) / half)
    ang = pos.astype(jnp.float32)[:, None] * inv[None, :]
    cos = jnp.cos(ang)[None, :, None, :]
    sin = jnp.sin(ang)[None, :, None, :]
    x1, x2 = x[..., :half], x[..., half:]
    return jnp.concatenate([x1 * cos - x2 * sin, x1 * sin + x2 * cos], axis=-1)


def retention_chunkwise(q, k, v, log_gamma):
    B, S, H, dk = q.shape
    dv = v.shape[-1]
    C = RET_CHUNK
    N = S // C
    qc = q.reshape(B, N, C, H, dk)
    kc = k.reshape(B, N, C, H, dk)
    vc = v.reshape(B, N, C, H, dv)
    idx = jnp.arange(C, dtype=jnp.float32)
    diff = idx[:, None] - idx[None, :]
    decay = jnp.where(diff[None] >= 0,
                      jnp.exp(log_gamma[:, None, None] * jnp.maximum(diff, 0.0)[None]), 0.0)
    scores = jnp.einsum('bnihd,bnjhd->bnhij', qc, kc) * decay[None, None]
    intra = jnp.einsum('bnhij,bnjhv->bnihv', scores, vc)
    q_dec = jnp.exp(log_gamma[:, None] * (idx + 1.0)[None, :]).T[None, :, :, None]
    k_dec = jnp.exp(log_gamma[:, None] * (C - 1.0 - idx)[None, :]).T[None, :, :, None]
    chunk_dec = jnp.exp(log_gamma * C)[None, :, None, None]

    def step(state, inp):
        qi, ki, vi = inp
        cross = jnp.einsum('bihd,bhdv->bihv', qi * q_dec, state)
        new_state = state * chunk_dec + jnp.einsum('bjhd,bjhv->bhdv', ki * k_dec, vi)
        return new_state, cross

    init = jnp.zeros((B, H, dk, dv), jnp.float32)
    _, cross = lax.scan(step, init, (jnp.moveaxis(qc, 1, 0), jnp.moveaxis(kc, 1, 0), jnp.moveaxis(vc, 1, 0)))
    out = intra + jnp.moveaxis(cross, 0, 1)
    return out.reshape(B, S, H, dv)


def sliding_window_gqa_sinks(q, k, v, sinks):
    B, S, Hq, d = q.shape
    Hkv = k.shape[2]
    G = Hq // Hkv
    C = SWA_BLOCK
    N = S // C
    scale = 1.0 / math.sqrt(d)
    qb = jnp.moveaxis(q.reshape(B, N, C, Hkv, G, d), 1, 0)

    def band(t):
        tb = t.reshape(B, N, C, Hkv, d)
        prev = jnp.concatenate([jnp.zeros_like(tb[:, :1]), tb[:, :-1]], axis=1)
        return jnp.moveaxis(jnp.concatenate([prev, tb], axis=2), 1, 0)

    kband, vband = band(k), band(v)
    qi = jnp.arange(C)
    kj = jnp.arange(2 * C)
    rel = C + qi[:, None] - kj[None, :]
    in_window = (rel >= 0) & (rel < SWA_WINDOW)
    sink = sinks.astype(jnp.float32).reshape(Hkv, G)[None, :, :, None]

    def block(inp):
        n, qn, kn, vn = inp
        s = jnp.einsum('bihgd,bjhd->bhgij', qn, kn).astype(jnp.float32) * scale
        valid = in_window & ((n * C - C + kj) >= 0)[None, :]
        s = jnp.where(valid, s, NEG_INF)
        m = jnp.maximum(jnp.max(s, axis=-1), sink)
        p = jnp.exp(s - m[..., None])
        denom = jnp.sum(p, axis=-1) + jnp.exp(sink - m)
        o = jnp.einsum('bhgij,bjhd->bihgd', p / denom[..., None], vn.astype(jnp.float32))
        return o.astype(qn.dtype)

    out = lax.map(block, (jnp.arange(N), qb, kband, vband))
    return jnp.moveaxis(out, 0, 1).reshape(B, S, Hq * d)


def memory_cross_attention(h, m, w_xq, w_xkv, w_xo):
    B, S, _ = h.shape
    q = (h @ w_xq).reshape(B, S, XATTN_HEADS, XATTN_HEAD_DIM)
    k, v = jnp.split(m @ w_xkv, 2, axis=-1)
    k = k.reshape(B, -1, XATTN_HEADS, XATTN_HEAD_DIM)
    v = v.reshape(B, -1, XATTN_HEADS, XATTN_HEAD_DIM)
    s = jnp.einsum('bshd,bmhd->bhsm', q, k).astype(jnp.float32) / math.sqrt(XATTN_HEAD_DIM)
    p = jax.nn.softmax(s, axis=-1).astype(v.dtype)
    o = jnp.einsum('bhsm,bmhd->bshd', p, v).reshape(B, S, XATTN_W)
    return o @ w_xo


def hierarchical_moe(h, w_rg, b_rg, w_re, b_re, w_gate, w_up, w_down):
    B, S, D = h.shape
    T = B * S
    xt = h.reshape(T, D)
    g_prob = jax.nn.softmax((xt @ w_rg).astype(jnp.float32) + b_rg.astype(jnp.float32), axis=-1)
    g_top, g_idx = lax.top_k(g_prob, 1)
    e_logits = ((xt @ w_re).astype(jnp.float32) + b_re.astype(jnp.float32)).reshape(T, N_GROUPS, EXPERTS_PER_GROUP)
    in_group = jnp.take_along_axis(e_logits, g_idx[:, :, None], axis=1)[:, 0]
    top_logit, top_local = lax.top_k(in_group, TOP_K)
    top_w = jax.nn.softmax(top_logit, axis=-1) * g_top
    expert = g_idx * EXPERTS_PER_GROUP + top_local

    A = T * TOP_K
    e_flat = expert.reshape(A)
    tok_flat = jnp.repeat(jnp.arange(T, dtype=jnp.int32), TOP_K)
    order = jnp.argsort(e_flat)
    e_sorted = e_flat[order]
    counts = jnp.zeros((N_EXPERTS,), jnp.int32).at[e_flat].add(1)
    padded = ((counts + MOE_BLOCK - 1) // MOE_BLOCK) * MOE_BLOCK
    start = jnp.cumsum(counts) - counts
    pend = jnp.cumsum(padded)
    pstart = pend - padded
    ppos_sorted = pstart[e_sorted] + jnp.arange(A, dtype=jnp.int32) - start[e_sorted]
    P = A + N_EXPERTS * MOE_BLOCK
    nb = P // MOE_BLOCK
    buf_tok = jnp.full((P,), T, jnp.int32).at[ppos_sorted].set(tok_flat[order])
    block_expert = jnp.minimum(
        jnp.searchsorted(pend, jnp.arange(nb, dtype=jnp.int32) * MOE_BLOCK, side='right'), N_EXPERTS - 1)
    x_pad = jnp.concatenate([xt, jnp.zeros((1, D), xt.dtype)], axis=0)

    def run_block(inp):
        tok, e = inp
        xb = x_pad[tok]
        return (jax.nn.silu(xb @ w_gate[e]) * (xb @ w_up[e])) @ w_down[e]

    out = lax.map(run_block, (buf_tok.reshape(nb, MOE_BLOCK), block_expert)).reshape(P, D)
    assign_pos = jnp.zeros((A,), jnp.int32).at[order].set(ppos_sorted)
    y = jnp.einsum('tk,tkd->td', top_w.astype(out.dtype), out[assign_pos].reshape(T, TOP_K, D))
    return y.reshape(B, S, D)


def setup_inputs(seed: int = 0) -> dict:
    key = jax.random.key(seed)
    ks = jax.random.split(key, 24)
    f32 = jnp.float32

    def nrm(k, shape, fan_in, mult=1.0):
        return jax.random.normal(k, shape, f32) * (mult * fan_in ** -0.5)

    def gain(k, shape):
        return 1.0 + 0.02 * jax.random.normal(k, shape, f32)

    return {
        "x": jax.random.normal(ks[0], (BATCH, SEQ, D_MODEL), f32),
        "mem": jax.random.normal(ks[1], (BATCH, MEM_LEN, D_MODEL), f32),
        "mix_norm_g": gain(ks[2], (DEPTH, D_MODEL)),
        "w_in": nrm(ks[3], (DEPTH, D_MODEL, IN_WIDTH), D_MODEL),
        "ret_norm_g": gain(ks[4], (DEPTH, RET_HEADS, RET_DV)),
        "swa_sinks": 0.5 * jax.random.normal(ks[5], (DEPTH, SWA_Q_HEADS), f32),
        "w_ret_o": nrm(ks[6], (DEPTH, RET_V, D_MODEL), RET_V),
        "w_swa_o": nrm(ks[7], (DEPTH, SWA_Q, D_MODEL), SWA_Q),
        "w_mix_o": nrm(ks[8], (DEPTH, D_MODEL, D_MODEL), D_MODEL),
        "xattn_norm_g": gain(ks[9], (DEPTH, D_MODEL)),
        "mem_norm_g": gain(ks[10], (DEPTH, D_MODEL)),
        "w_xq": nrm(ks[11], (DEPTH, D_MODEL, XATTN_W), D_MODEL),
        "w_xkv": nrm(ks[12], (DEPTH, D_MODEL, 2 * XATTN_W), D_MODEL),
        "w_xo": nrm(ks[13], (DEPTH, XATTN_W, D_MODEL), XATTN_W),
        "moe_norm_g": gain(ks[14], (DEPTH, D_MODEL)),
        "w_router_group": nrm(ks[15], (DEPTH, D_MODEL, N_GROUPS), D_MODEL),
        "b_router_group": 0.01 * jax.random.normal(ks[16], (DEPTH, N_GROUPS), f32),
        "w_router_expert": nrm(ks[17], (DEPTH, D_MODEL, N_EXPERTS), D_MODEL),
        "b_router_expert": 0.01 * jax.random.normal(ks[18], (DEPTH, N_EXPERTS), f32),
        "w_exp_gate": nrm(ks[19], (DEPTH, N_EXPERTS, D_MODEL, D_EXPERT), D_MODEL),
        "w_exp_up": nrm(ks[20], (DEPTH, N_EXPERTS, D_MODEL, D_EXPERT), D_MODEL),
        "w_exp_down": nrm(ks[21], (DEPTH, N_EXPERTS, D_EXPERT, D_MODEL), D_EXPERT),
        "final_norm_g": gain(ks[22], (D_MODEL,)),
    }


def reference(x, mem, mix_norm_g, w_in, ret_norm_g, swa_sinks, w_ret_o, w_swa_o, w_mix_o,
              xattn_norm_g, mem_norm_g, w_xq, w_xkv, w_xo, moe_norm_g,
              w_router_group, b_router_group, w_router_expert, b_router_expert,
              w_exp_gate, w_exp_up, w_exp_down, final_norm_g):
    B, S, _ = x.shape
    pos = jnp.arange(S, dtype=jnp.int32)
    log_gamma = jnp.log1p(-jnp.exp2(-5.0 - jnp.arange(RET_HEADS, dtype=jnp.float32)))
    for l in range(DEPTH):
        h = rms_norm(x, mix_norm_g[l])
        proj = h @ w_in[l]
        rq, rk, rv, rg, sq, sk, sv, a_ret, a_swa = jnp.split(proj, IN_SPLITS, axis=-1)
        q_r = rotary(rq.reshape(B, S, RET_HEADS, RET_DK).astype(jnp.float32), pos)
        k_r = rotary(rk.reshape(B, S, RET_HEADS, RET_DK).astype(jnp.float32), pos) * (RET_DK ** -0.5)
        v_r = rv.reshape(B, S, RET_HEADS, RET_DV).astype(jnp.float32)
        ret = retention_chunkwise(q_r, k_r, v_r, log_gamma)
        ret = rms_norm(ret, ret_norm_g[l]).reshape(B, S, RET_V)
        ret = (jax.nn.silu(rg.astype(jnp.float32)) * ret).astype(x.dtype)
        swa = sliding_window_gqa_sinks(sq.reshape(B, S, SWA_Q_HEADS, SWA_HEAD_DIM),
                                       sk.reshape(B, S, SWA_KV_HEADS, SWA_HEAD_DIM),
                                       sv.reshape(B, S, SWA_KV_HEADS, SWA_HEAD_DIM),
                                       swa_sinks[l])
        merged = jax.nn.sigmoid(a_ret) * (ret @ w_ret_o[l]) + jax.nn.sigmoid(a_swa) * (swa @ w_swa_o[l])
        x = x + merged @ w_mix_o[l]
        x = x + memory_cross_attention(rms_norm(x, xattn_norm_g[l]), rms_norm(mem, mem_norm_g[l]),
                                       w_xq[l], w_xkv[l], w_xo[l])
        x = x + hierarchical_moe(rms_norm(x, moe_norm_g[l]), w_router_group[l], b_router_group[l],
                                 w_router_expert[l], b_router_expert[l],
                                 w_exp_gate[l], w_exp_up[l], w_exp_down[l])
    return rms_norm(x, final_norm_g)
```

```python
import functools
import math

import jax
import jax.numpy as jnp
from jax import lax
from jax.experimental import pallas as pl
from jax.experimental.pallas import tpu as pltpu

F32 = jnp.float32
BF16 = jnp.bfloat16

RMS_EPS = 1e-6
ROPE_BASE = 10000.0
NEG_INF = -1e30
RET_HEADS, RET_DK, RET_DV, RET_CHUNK = 16, 128, 256, 128
SWA_Q_HEADS, SWA_KV_HEADS, SWA_HEAD_DIM, SWA_WINDOW, SWA_BLOCK = 64, 8, 64, 128, 128
XATTN_HEADS, XATTN_HEAD_DIM = 4, 256
N_GROUPS, EXPERTS_PER_GROUP, TOP_K, D_EXPERT = 8, 8, 2, 512
N_EXPERTS = N_GROUPS * EXPERTS_PER_GROUP
RET_QK = RET_HEADS * RET_DK
RET_V = RET_HEADS * RET_DV
SWA_Q = SWA_Q_HEADS * SWA_HEAD_DIM
SWA_KV = SWA_KV_HEADS * SWA_HEAD_DIM
XATTN_W = XATTN_HEADS * XATTN_HEAD_DIM
COL_RQ = 0
COL_RK = COL_RQ + RET_QK
COL_RV = COL_RK + RET_QK
COL_RG = COL_RV + RET_V
COL_SQ = COL_RG + RET_V
COL_SK = COL_SQ + SWA_Q
COL_SV = COL_SK + SWA_KV
COL_AR = COL_SV + SWA_KV

V7X_LANES = 128
V7X_VMEM_LIMIT_BYTES = 56000 * 1024
ROUTE_LANES = V7X_LANES
MOE_ROWS = 256


def _params(sem, vmem=None):
    return pltpu.CompilerParams(dimension_semantics=sem, vmem_limit_bytes=vmem)


def _rms(x, g):
    ms = jnp.mean(x * x, axis=-1, keepdims=True)
    return x * lax.rsqrt(ms + RMS_EPS) * g


def _rmsnorm_kernel(x_ref, g_ref, o_ref):
    o_ref[...] = _rms(x_ref[...], g_ref[...]).astype(o_ref.dtype)


def _rmsnorm_cast(x, g, tm):
    T, D = x.shape
    return pl.pallas_call(
        _rmsnorm_kernel,
        grid=(T // tm,),
        in_specs=[pl.BlockSpec((tm, D), lambda i: (i, 0)), pl.BlockSpec((1, D), lambda i: (0, 0))],
        out_specs=pl.BlockSpec((tm, D), lambda i: (i, 0)),
        out_shape=jax.ShapeDtypeStruct((T, D), BF16),
        compiler_params=_params(("parallel",)),
        name="rmsnorm_cast",
    )(x, g.reshape(1, D))


def _inproj_kernel(a_ref, w_ref, o_ref, wbf_ref):
    @pl.when(pl.program_id(1) == 0)
    def _():
        wbf_ref[...] = w_ref[...].astype(BF16)

    o_ref[...] = jnp.dot(a_ref[...], wbf_ref[...], preferred_element_type=F32).astype(o_ref.dtype)


def _in_proj(h, w, tm, tn):
    T, K = h.shape
    N = w.shape[1]
    return pl.pallas_call(
        _inproj_kernel,
        grid=(N // tn, T // tm),
        in_specs=[pl.BlockSpec((tm, K), lambda j, i: (i, 0)), pl.BlockSpec((K, tn), lambda j, i: (0, j))],
        out_specs=pl.BlockSpec((tm, tn), lambda j, i: (i, j)),
        out_shape=jax.ShapeDtypeStruct((T, N), BF16),
        scratch_shapes=[pltpu.VMEM((K, tn), BF16)],
        compiler_params=_params(("arbitrary", "arbitrary"), V7X_VMEM_LIMIT_BYTES),
        name="in_proj",
    )(h, w)


def _retention_kernel(q_ref, k_ref, v_ref, g_ref, cos_ref, sin_ref, dec_ref, qd_ref, kd_ref, cd_ref, gn_ref,
                      o_ref, state_ref, *, nchunk):
    C = RET_CHUNK

    @pl.when(pl.program_id(2) == 0)
    def _():
        state_ref[...] = jnp.zeros_like(state_ref)

    dec = dec_ref[...]
    qd = qd_ref[...]
    kd = kd_ref[...]
    cd = cd_ref[0:1, :]
    gn = gn_ref[...]
    nt = (((1,), (1,)), ((), ()))
    for c in range(nchunk):
        sl = pl.ds(c * C, C)
        cos = cos_ref[sl, :]
        sin = sin_ref[sl, :]
        q = q_ref[sl, :].astype(F32)
        k = k_ref[sl, :].astype(F32)
        q = q * cos + pltpu.roll(q, RET_DK // 2, 1) * sin
        k = (k * cos + pltpu.roll(k, RET_DK // 2, 1) * sin) * (RET_DK ** -0.5)
        v = v_ref[sl, :]
        s = lax.dot_general(q.astype(BF16), k.astype(BF16), nt, preferred_element_type=F32) * dec
        intra = jnp.dot(s.astype(BF16), v, preferred_element_type=F32)
        state = state_ref[...]
        cross = jnp.dot((q * qd).astype(BF16), state.astype(BF16), preferred_element_type=F32)
        kdt = jnp.transpose(k * kd)
        state_ref[...] = state * cd + jnp.dot(kdt.astype(BF16), v, preferred_element_type=F32)
        out = intra + cross
        y = _rms(out, gn)
        g = g_ref[sl, :].astype(F32)
        o_ref[sl, :] = (g * jax.nn.sigmoid(g) * y).astype(o_ref.dtype)


def _retention(proj, ret_norm_g, B, S, ts):
    T = B * S
    C = RET_CHUNK
    H = RET_HEADS
    ns = S // ts
    half = RET_DK // 2
    inv = ROPE_BASE ** (-jnp.arange(half, dtype=F32) / half)
    ang = jnp.arange(S, dtype=F32)[:, None] * inv[None, :]
    cos = jnp.cos(ang)
    sin = jnp.sin(ang)
    cosf = jnp.concatenate([cos, cos], axis=-1)
    sinf = jnp.concatenate([-sin, sin], axis=-1)
    log_gamma = jnp.log1p(-jnp.exp2(-5.0 - jnp.arange(H, dtype=F32)))
    idx = jnp.arange(C, dtype=F32)
    diff = idx[:, None] - idx[None, :]
    dec = jnp.where(diff[None] >= 0, jnp.exp(log_gamma[:, None, None] * jnp.maximum(diff, 0.0)[None]), 0.0)
    qd = jnp.broadcast_to(jnp.exp(log_gamma[:, None] * (idx + 1.0)[None, :])[:, :, None], (H, C, RET_DK))
    kd = jnp.broadcast_to(jnp.exp(log_gamma[:, None] * (C - 1.0 - idx)[None, :])[:, :, None], (H, C, RET_DK))
    cd = jnp.broadcast_to(jnp.exp(log_gamma * C)[:, None, None], (H, 8, RET_DV))
    gn = ret_norm_g.reshape(H, 1, RET_DV)

    qb, kb = COL_RQ // RET_DK, COL_RK // RET_DK
    vb, gb = COL_RV // RET_DV, COL_RG // RET_DV
    row = lambda b, h, s: b * ns + s
    head3 = lambda b, h, s: (h, 0, 0)
    return pl.pallas_call(
        functools.partial(_retention_kernel, nchunk=ts // C),
        grid=(B, H, ns),
        in_specs=[
            pl.BlockSpec((ts, RET_DK), lambda b, h, s: (row(b, h, s), qb + h)),
            pl.BlockSpec((ts, RET_DK), lambda b, h, s: (row(b, h, s), kb + h)),
            pl.BlockSpec((ts, RET_DV), lambda b, h, s: (row(b, h, s), vb + h)),
            pl.BlockSpec((ts, RET_DV), lambda b, h, s: (row(b, h, s), gb + h)),
            pl.BlockSpec((ts, RET_DK), lambda b, h, s: (s, 0)),
            pl.BlockSpec((ts, RET_DK), lambda b, h, s: (s, 0)),
            pl.BlockSpec((None, C, C), head3),
            pl.BlockSpec((None, C, RET_DK), head3),
            pl.BlockSpec((None, C, RET_DK), head3),
            pl.BlockSpec((None, 8, RET_DV), head3),
            pl.BlockSpec((None, 1, RET_DV), head3),
        ],
        out_specs=pl.BlockSpec((ts, RET_DV), lambda b, h, s: (row(b, h, s), h)),
        out_shape=jax.ShapeDtypeStruct((T, RET_V), BF16),
        scratch_shapes=[pltpu.VMEM((RET_DK, RET_DV), F32)],
        compiler_params=_params(("arbitrary", "arbitrary", "arbitrary")),
        name="retention",
    )(proj, proj, proj, proj, cosf, sinf, dec, qd, kd, cd, gn)


def _swa_kernel(sinks_ref, q_ref, kp_ref, kc_ref, vp_ref, vc_ref, o_ref):
    C = SWA_BLOCK
    G = SWA_Q_HEADS // SWA_KV_HEADS
    npair = G // 2
    d = SWA_HEAD_DIM
    n = pl.program_id(1)
    rows = npair * C
    row = lax.broadcasted_iota(jnp.int32, (rows, 2 * C), 0) & (C - 1)
    col = lax.broadcasted_iota(jnp.int32, (rows, 2 * C), 1)
    rel = C + row - col
    first_key = jnp.where(n > 0, 0, C)
    valid = (rel >= 0) & (rel < SWA_WINDOW) & (col >= first_key)
    lane = lax.broadcasted_iota(jnp.int32, (2 * C, 2 * d), 1)
    scale = 1.0 / math.sqrt(d)
    nt = (((1,), (1,)), ((), ()))

    def halves(prev_ref, cur_ref, h, mult):
        t = h // 2
        band = jnp.concatenate([prev_ref[:, t * 2 * d:(t + 1) * 2 * d], cur_ref[:, t * 2 * d:(t + 1) * 2 * d]],
                               axis=0).astype(F32) * mult
        if h % 2 == 0:
            lo = jnp.where(lane < d, band, 0.0)
            hi = pltpu.roll(lo, d, 1)
        else:
            hi = jnp.where(lane >= d, band, 0.0)
            lo = pltpu.roll(hi, d, 1)
        return jnp.concatenate([lo, hi], axis=0).astype(BF16)

    for h in range(SWA_KV_HEADS):
        k2 = halves(kp_ref, kc_ref, h, scale)
        v2 = halves(vp_ref, vc_ref, h, 1.0)
        qg = jnp.concatenate([q_ref[:, (h * npair + p) * 2 * d:(h * npair + p + 1) * 2 * d] for p in range(npair)],
                             axis=0)
        s = lax.dot_general(qg, k2, nt, preferred_element_type=F32)
        ps = []
        for half in range(2):
            sh = jnp.where(valid, s[:, half * 2 * C:(half + 1) * 2 * C], NEG_INF)
            sink = jnp.concatenate(
                [jnp.full((C, 1), sinks_ref[h * G + p * 2 + half], F32) for p in range(npair)], axis=0)
            m = jnp.maximum(jnp.max(sh, axis=-1, keepdims=True), sink)
            p_ = jnp.exp(sh - m)
            denom = jnp.sum(p_, axis=-1, keepdims=True) + jnp.exp(sink - m)
            ps.append((p_ / denom).astype(BF16))
        o = jnp.dot(jnp.concatenate(ps, axis=1), v2, preferred_element_type=F32)
        for p in range(npair):
            o_ref[:, (h * npair + p) * 2 * d:(h * npair + p + 1) * 2 * d] = o[p * C:(p + 1) * C, :].astype(o_ref.dtype)


def _swa(proj, sinks, B, S):
    T = B * S
    C = SWA_BLOCK
    nb = S // C
    qb = COL_SQ // SWA_Q
    kb = COL_SK // SWA_KV
    vb = COL_SV // SWA_KV
    cur = lambda b, n: b * nb + n
    prev = lambda b, n: b * nb + jnp.maximum(n - 1, 0)
    return pl.pallas_call(
        _swa_kernel,
        grid=(B, nb),
        in_specs=[
            pl.BlockSpec(memory_space=pltpu.SMEM),
            pl.BlockSpec((C, SWA_Q), lambda b, n: (cur(b, n), qb)),
            pl.BlockSpec((C, SWA_KV), lambda b, n: (prev(b, n), kb)),
            pl.BlockSpec((C, SWA_KV), lambda b, n: (cur(b, n), kb)),
            pl.BlockSpec((C, SWA_KV), lambda b, n: (prev(b, n), vb)),
            pl.BlockSpec((C, SWA_KV), lambda b, n: (cur(b, n), vb)),
        ],
        out_specs=pl.BlockSpec((C, SWA_Q), lambda b, n: (cur(b, n), 0)),
        out_shape=jax.ShapeDtypeStruct((T, SWA_Q), BF16),
        compiler_params=_params(("parallel", "parallel")),
        name="swa",
    )(sinks, proj, proj, proj, proj, proj)


def _merge_kernel(r_ref, s_ref, wr_ref, ws_ref, ar_ref, as_ref, o_ref):
    r = jnp.dot(r_ref[...], wr_ref[...], preferred_element_type=F32)
    s = jnp.dot(s_ref[...], ws_ref[...], preferred_element_type=F32)
    o_ref[...] = (jax.nn.sigmoid(ar_ref[...].astype(F32)) * r
                  + jax.nn.sigmoid(as_ref[...].astype(F32)) * s).astype(o_ref.dtype)


def _merge(ret, swa, w_ret_o, w_swa_o, proj, tm, tn):
    T, K = ret.shape
    N = w_ret_o.shape[1]
    arb = COL_AR // tn
    asb = (COL_AR + N) // tn
    return pl.pallas_call(
        _merge_kernel,
        grid=(T // tm, N // tn),
        in_specs=[
            pl.BlockSpec((tm, K), lambda i, j: (i, 0)),
            pl.BlockSpec((tm, K), lambda i, j: (i, 0)),
            pl.BlockSpec((K, tn), lambda i, j: (0, j)),
            pl.BlockSpec((K, tn), lambda i, j: (0, j)),
            pl.BlockSpec((tm, tn), lambda i, j: (i, arb + j)),
            pl.BlockSpec((tm, tn), lambda i, j: (i, asb + j)),
        ],
        out_specs=pl.BlockSpec((tm, tn), lambda i, j: (i, j)),
        out_shape=jax.ShapeDtypeStruct((T, N), BF16),
        compiler_params=_params(("parallel", "parallel"), V7X_VMEM_LIMIT_BYTES),
        name="merge",
    )(ret, swa, w_ret_o, w_swa_o, proj, proj)


def _resid_mm_kernel(a_ref, w_ref, x_ref, o_ref):
    o_ref[...] = x_ref[...] + jnp.dot(a_ref[...], w_ref[...], preferred_element_type=F32)


def _resid_mm(a, w, x, tm, tn):
    T, K = a.shape
    N = w.shape[1]
    return pl.pallas_call(
        _resid_mm_kernel,
        grid=(T // tm, N // tn),
        in_specs=[
            pl.BlockSpec((tm, K), lambda i, j: (i, 0)),
            pl.BlockSpec((K, tn), lambda i, j: (0, j)),
            pl.BlockSpec((tm, tn), lambda i, j: (i, j)),
        ],
        out_specs=pl.BlockSpec((tm, tn), lambda i, j: (i, j)),
        out_shape=jax.ShapeDtypeStruct((T, N), F32),
        compiler_params=_params(("parallel", "parallel"), V7X_VMEM_LIMIT_BYTES),
        name="mix_out",
    )(a, w, x)


def _norm_mm_kernel(x_ref, g_ref, w_ref, o_ref):
    h = _rms(x_ref[...], g_ref[...]).astype(BF16)
    o_ref[...] = jnp.dot(h, w_ref[...], preferred_element_type=F32).astype(o_ref.dtype)


def _norm_mm(x, g, w, tm, tn, name):
    T, K = x.shape
    N = w.shape[1]
    return pl.pallas_call(
        _norm_mm_kernel,
        grid=(T // tm, N // tn),
        in_specs=[
            pl.BlockSpec((tm, K), lambda i, j: (i, 0)),
            pl.BlockSpec((1, K), lambda i, j: (0, 0)),
            pl.BlockSpec((K, tn), lambda i, j: (0, j)),
        ],
        out_specs=pl.BlockSpec((tm, tn), lambda i, j: (i, j)),
        out_shape=jax.ShapeDtypeStruct((T, N), BF16),
        compiler_params=_params(("parallel", "parallel"), V7X_VMEM_LIMIT_BYTES),
        name=name,
    )(x, g.reshape(1, K), w)


def _xattn_kernel(q_ref, k_ref, v_ref, wo_ref, x_ref, o_ref):
    dh = XATTN_HEAD_DIM
    nt = (((1,), (1,)), ((), ()))
    outs = []
    for h in range(XATTN_HEADS):
        q = q_ref[:, h * dh:(h + 1) * dh]
        k = k_ref[:, h * dh:(h + 1) * dh]
        v = v_ref[:, h * dh:(h + 1) * dh]
        s = lax.dot_general(q, k, nt, preferred_element_type=F32) / math.sqrt(dh)
        m = jnp.max(s, axis=-1, keepdims=True)
        e = jnp.exp(s - m)
        p = (e / jnp.sum(e, axis=-1, keepdims=True)).astype(BF16)
        outs.append(jnp.dot(p, v, preferred_element_type=F32).astype(BF16))
    o = jnp.concatenate(outs, axis=1)
    o_ref[...] = x_ref[...] + jnp.dot(o, wo_ref[...], preferred_element_type=F32)


def _xattn(q, kv, w_xo, x, B, S, M, tm):
    T, D = x.shape
    ns = S // tm
    return pl.pallas_call(
        _xattn_kernel,
        grid=(B, ns),
        in_specs=[
            pl.BlockSpec((tm, XATTN_W), lambda b, s: (b * ns + s, 0)),
            pl.BlockSpec((M, XATTN_W), lambda b, s: (b, 0)),
            pl.BlockSpec((M, XATTN_W), lambda b, s: (b, 1)),
            pl.BlockSpec((XATTN_W, D), lambda b, s: (0, 0)),
            pl.BlockSpec((tm, D), lambda b, s: (b * ns + s, 0)),
        ],
        out_specs=pl.BlockSpec((tm, D), lambda b, s: (b * ns + s, 0)),
        out_shape=jax.ShapeDtypeStruct((T, D), F32),
        compiler_params=_params(("parallel", "parallel"), V7X_VMEM_LIMIT_BYTES),
        name="xattn",
    )(q, kv, kv, w_xo, x)


def _router_kernel(x_ref, g_ref, w_ref, b_ref, h_ref, r_ref):
    h = _rms(x_ref[...], g_ref[...])
    h_ref[...] = h
    logits = jnp.dot(h.astype(BF16), w_ref[...], preferred_element_type=F32) + b_ref[...]
    lane = lax.broadcasted_iota(jnp.int32, logits.shape, 1).astype(F32)
    big = float(ROUTE_LANES)

    def top(vals):
        t = jnp.max(vals, axis=-1, keepdims=True)
        i = jnp.min(jnp.where(vals == t, lane, big), axis=-1, keepdims=True)
        return t, i

    is_g = lane < N_GROUPS
    gmax, gidx = top(jnp.where(is_g, logits, -jnp.inf))
    gsum = jnp.sum(jnp.where(is_g, jnp.exp(logits - gmax), 0.0), axis=-1, keepdims=True)
    gtop = 1.0 / gsum
    lo = N_GROUPS + gidx * EXPERTS_PER_GROUP
    el = jnp.where((lane >= lo) & (lane < lo + EXPERTS_PER_GROUP), logits, -jnp.inf)
    t1, i1 = top(el)
    t2, i2 = top(jnp.where(lane == i1, -jnp.inf, el))
    e2 = jnp.exp(t2 - t1)
    den = 1.0 + e2
    w1 = gtop * (1.0 / den)
    w2 = gtop * (e2 / den)
    r_ref[...] = jnp.where(lane == 0, i1 - N_GROUPS,
                           jnp.where(lane == 1, i2 - N_GROUPS,
                                     jnp.where(lane == 2, w1, jnp.where(lane == 3, w2, 0.0))))


def _router(x, g, w_r, b_r, tm):
    T, D = x.shape
    return pl.pallas_call(
        _router_kernel,
        grid=(T // tm,),
        in_specs=[
            pl.BlockSpec((tm, D), lambda i: (i, 0)),
            pl.BlockSpec((1, D), lambda i: (0, 0)),
            pl.BlockSpec((D, ROUTE_LANES), lambda i: (0, 0)),
            pl.BlockSpec((1, ROUTE_LANES), lambda i: (0, 0)),
        ],
        out_specs=[pl.BlockSpec((tm, D), lambda i: (i, 0)), pl.BlockSpec((tm, ROUTE_LANES), lambda i: (i, 0))],
        out_shape=[jax.ShapeDtypeStruct((T, D), F32), jax.ShapeDtypeStruct((T, ROUTE_LANES), F32)],
        compiler_params=_params(("parallel",), V7X_VMEM_LIMIT_BYTES),
        name="router",
    )(x, g.reshape(1, D), w_r, b_r)


def _moe_kernel(bexp_ref, nused_ref, dst_ref, hm_hbm, wg_ref, wu_ref, wd_ref, out_hbm, xbuf, obuf, sem_in, sem_out,
                *, tm, T):
    r = pl.program_id(0)
    base = r * tm

    def gather(i):
        d = dst_ref[base + i]
        tok = jnp.where(d >= 0, d & (T - 1), 0)
        return pltpu.make_async_copy(hm_hbm.at[pl.ds(tok, 1)], xbuf.at[pl.ds(i, 1)], sem_in)

    def scatter(i, d):
        return pltpu.make_async_copy(obuf.at[pl.ds(i, 1)], out_hbm.at[pl.ds(d, 1)], sem_out)

    def for_rows(fn):
        def body(i, c):
            fn(i)
            return c

        lax.fori_loop(0, tm, body, 0)

    def for_real_rows(fn):
        def body(i, c):
            d = dst_ref[base + i]

            @pl.when(d >= 0)
            def _():
                fn(i, d)

            return c

        lax.fori_loop(0, tm, body, 0)

    @pl.when(r < nused_ref[0])
    def _():
        for_rows(lambda i: gather(i).start())
        for_rows(lambda i: gather(i).wait())
        x = xbuf[...].astype(BF16)
        g = jnp.dot(x, wg_ref[...], preferred_element_type=F32)
        u = jnp.dot(x, wu_ref[...], preferred_element_type=F32)
        a = (g * jax.nn.sigmoid(g) * u).astype(BF16)
        obuf[...] = jnp.dot(a, wd_ref[...], preferred_element_type=F32)
        for_real_rows(lambda i, d: scatter(i, d).start())
        for_real_rows(lambda i, d: scatter(i, d).wait())


def _moe(hm, w_gate, w_up, w_down, bexp, nused, dst, tm):
    T, D = hm.shape
    assert T & (T - 1) == 0, "dst packs (slot, token) as slot * T + token and unpacks the token with a mask"
    nblk = bexp.shape[0]
    wmap = lambda r, bexp, nused, dst: (bexp[r], 0, 0)
    return pl.pallas_call(
        functools.partial(_moe_kernel, tm=tm, T=T),
        grid_spec=pltpu.PrefetchScalarGridSpec(
            num_scalar_prefetch=3,
            grid=(nblk,),
            in_specs=[
                pl.BlockSpec(memory_space=pl.ANY),
                pl.BlockSpec((None, D, D_EXPERT), wmap),
                pl.BlockSpec((None, D, D_EXPERT), wmap),
                pl.BlockSpec((None, D_EXPERT, D), wmap),
            ],
            out_specs=pl.BlockSpec(memory_space=pl.ANY),
            scratch_shapes=[pltpu.VMEM((tm, D), F32), pltpu.VMEM((tm, D), F32),
                            pltpu.SemaphoreType.DMA(()), pltpu.SemaphoreType.DMA(())],
        ),
        out_shape=jax.ShapeDtypeStruct((TOP_K * T, D), F32),
        compiler_params=_params(("arbitrary",), V7X_VMEM_LIMIT_BYTES),
        name="moe_experts",
    )(bexp, nused, dst, hm, w_gate, w_up, w_down)


def _moe_plan(route, T, tm):
    A = T * TOP_K
    e_flat = route[:, :TOP_K].astype(jnp.int32).reshape(A)
    onehot = (e_flat[:, None] == jnp.arange(N_EXPERTS, dtype=jnp.int32)[None, :]).astype(jnp.int32)
    csum = jnp.cumsum(onehot, axis=0)
    rank = jnp.sum((csum - onehot) * onehot, axis=1)
    counts = csum[-1]
    padded = ((counts + tm - 1) // tm) * tm
    pend = jnp.cumsum(padded)
    pstart = pend - padded
    ppos = pstart[e_flat] + rank
    P = A + N_EXPERTS * tm
    nblk = P // tm
    a = jnp.arange(A, dtype=jnp.int32)
    dst = jnp.full((P,), -1, jnp.int32).at[ppos].set((a % TOP_K) * T + a // TOP_K)
    bexp = jnp.minimum(jnp.searchsorted(pend, jnp.arange(nblk, dtype=jnp.int32) * tm, side="right"),
                       N_EXPERTS - 1).astype(jnp.int32)
    nused = (pend[-1] // tm).astype(jnp.int32).reshape(1)
    return bexp, nused, dst


def _final_kernel(x_ref, o0_ref, o1_ref, r_ref, g_ref, o_ref):
    r = r_ref[...]
    y = x_ref[...] + r[:, 2:3] * o0_ref[...] + r[:, 3:4] * o1_ref[...]
    o_ref[...] = _rms(y, g_ref[...])


def _final(x, out2, route, g, tm):
    T, D = x.shape
    nt = T // tm
    return pl.pallas_call(
        _final_kernel,
        grid=(nt,),
        in_specs=[
            pl.BlockSpec((tm, D), lambda i: (i, 0)),
            pl.BlockSpec((tm, D), lambda i: (i, 0)),
            pl.BlockSpec((tm, D), lambda i: (nt + i, 0)),
            pl.BlockSpec((tm, ROUTE_LANES), lambda i: (i, 0)),
            pl.BlockSpec((1, D), lambda i: (0, 0)),
        ],
        out_specs=pl.BlockSpec((tm, D), lambda i: (i, 0)),
        out_shape=jax.ShapeDtypeStruct((T, D), F32),
        compiler_params=_params(("parallel",), V7X_VMEM_LIMIT_BYTES),
        name="final_combine",
    )(x, out2, out2, route, g.reshape(1, D))


def _layer(x, mem, B, S, mix_norm_g, w_in, ret_norm_g, swa_sinks, w_ret_o, w_swa_o, w_mix_o, xattn_norm_g,
           mem_norm_g, w_xq, w_xkv, w_xo, moe_norm_g, w_router_group, b_router_group, w_router_expert,
           b_router_expert, w_exp_gate, w_exp_up, w_exp_down):
    T, D = x.shape
    M = mem.shape[0] // B
    tm_big = min(1024, T)
    tm_mid = min(512, T)
    tm_small = min(256, S)
    h = _rmsnorm_cast(x, mix_norm_g, tm_mid)
    proj = _in_proj(h, w_in, tm_big, 512)
    ret = _retention(proj, ret_norm_g, B, S, min(512, S))
    swa = _swa(proj, swa_sinks, B, S)
    merged = _merge(ret, swa, w_ret_o.astype(BF16), w_swa_o.astype(BF16), proj, tm_mid, 512)
    x = _resid_mm(merged, w_mix_o.astype(BF16), x, tm_big, 512)
    kv = _norm_mm(mem, mem_norm_g, w_xkv.astype(BF16), mem.shape[0], 512, "mem_kv")
    q = _norm_mm(x, xattn_norm_g, w_xq.astype(BF16), tm_mid, XATTN_W, "xattn_q")
    x = _xattn(q, kv, w_xo.astype(BF16), x, B, S, M, tm_small)
    w_r = jnp.zeros((D, ROUTE_LANES), F32).at[:, :N_GROUPS].set(w_router_group)
    w_r = w_r.at[:, N_GROUPS:N_GROUPS + N_EXPERTS].set(w_router_expert).astype(BF16)
    b_r = jnp.zeros((1, ROUTE_LANES), F32).at[0, :N_GROUPS].set(b_router_group)
    b_r = b_r.at[0, N_GROUPS:N_GROUPS + N_EXPERTS].set(b_router_expert)
    hm, route = _router(x, moe_norm_g, w_r, b_r, tm_mid)
    bexp, nused, dst = _moe_plan(route, T, MOE_ROWS)
    out2 = _moe(hm, w_exp_gate.astype(BF16), w_exp_up.astype(BF16), w_exp_down.astype(BF16), bexp, nused, dst,
                MOE_ROWS)
    return x, out2, route


def kernel(x, mem, mix_norm_g, w_in, ret_norm_g, swa_sinks, w_ret_o, w_swa_o, w_mix_o, xattn_norm_g, mem_norm_g,
           w_xq, w_xkv, w_xo, moe_norm_g, w_router_group, b_router_group, w_router_expert, b_router_expert,
           w_exp_gate, w_exp_up, w_exp_down, final_norm_g):
    B, S, D = x.shape
    depth = mix_norm_g.shape[0]
    assert depth == 1, "the fused final combine assumes a single layer"
    T = B * S
    xt = x.reshape(T, D)
    memt = mem.reshape(-1, D)
    l = 0
    xt, out2, route = _layer(
        xt, memt, B, S, mix_norm_g[l], w_in[l], ret_norm_g[l], swa_sinks[l], w_ret_o[l], w_swa_o[l], w_mix_o[l],
        xattn_norm_g[l], mem_norm_g[l], w_xq[l], w_xkv[l], w_xo[l], moe_norm_g[l], w_router_group[l],
        b_router_group[l], w_router_expert[l], b_router_expert[l], w_exp_gate[l], w_exp_up[l], w_exp_down[l])
    out = _final(xt, out2, route, final_norm_g, min(256, T))
    return out.reshape(B, S, D)
```
